```python
import math
import jax, jax.numpy as jnp
from jax import lax
import numpy as np


D_MODEL = 1024
BATCH = 4
SEQ = 8192
DEPTH = 2

PLE_DIM = 256
N_BRANCH = 3
CONV_DIM = 512
CONV_WIDTH = 3
SB_HEADS = 8
SB_HEAD_DIM = 64
DIFF_HEADS = 4
DIFF_HEAD_DIM = 64
DIFF_V_DIM = 2 * DIFF_HEAD_DIM
ROPE_THETA = 10000.0
Q_BLOCK = 128
EPS = 1e-6
IN_SIZES = (CONV_DIM, CONV_DIM, CONV_DIM, CONV_DIM,
            SB_HEADS * SB_HEAD_DIM, SB_HEADS * SB_HEAD_DIM, SB_HEADS * SB_HEAD_DIM, SB_HEADS * SB_HEAD_DIM,
            DIFF_HEADS * 2 * DIFF_HEAD_DIM, DIFF_HEADS * 2 * DIFF_HEAD_DIM,
            DIFF_HEADS * DIFF_V_DIM, DIFF_HEADS * DIFF_V_DIM,
            N_BRANCH * D_MODEL)
IN_W = sum(IN_SIZES)

kernel_name = 'hybrid_conv_stickbreak_diffattn'


def _rmsnorm(x, w):
    xf = x.astype(jnp.float32)
    y = xf * lax.rsqrt(jnp.mean(xf * xf, axis=-1, keepdims=True) + EPS)
    return (y * w.astype(jnp.float32)).astype(x.dtype)


def _split_points():
    pts, acc = [], 0
    for s in IN_SIZES[:-1]:
        acc += s
        pts.append(acc)
    return pts


def _rope_tables(seq, dim, dtype):
    pos = jnp.arange(seq, dtype=jnp.float32)
    inv = 1.0 / (ROPE_THETA ** (jnp.arange(0, dim, 2, dtype=jnp.float32) / dim))
    ang = pos[:, None] * inv[None, :]
    return jnp.cos(ang).astype(dtype), jnp.sin(ang).astype(dtype)


def _rope(x, cos, sin):
    half = x.shape[-1] // 2
    x1, x2 = x[..., :half], x[..., half:]
    return jnp.concatenate([x1 * cos - x2 * sin, x1 * sin + x2 * cos], axis=-1)


def _causal_conv(v, w):
    width, seq = w.shape[0], v.shape[1]
    vp = jnp.pad(v, ((0, 0), (width - 1, 0), (0, 0)))
    out = w[0] * vp[:, 0:seq]
    for j in range(1, width):
        out = out + w[j] * vp[:, j:j + seq]
    return out


def _to_blocks(q):
    lead, s, d = q.shape[:-2], q.shape[-2], q.shape[-1]
    qb = q.reshape(*lead, s // Q_BLOCK, Q_BLOCK, d)
    return jnp.moveaxis(qb, -3, 0)


def _from_blocks(o):
    o = jnp.moveaxis(o, 0, -3)
    return o.reshape(*o.shape[:-3], o.shape[-3] * o.shape[-2], o.shape[-1])


def _stick_breaking_attention(q, k, v):
    seq = k.shape[-2]
    scale = q.shape[-1] ** -0.5
    kpos = jnp.arange(seq)

    def block(args):
        qi, bi = args
        z = jnp.einsum('bhqd,bhkd->bhqk', qi, k).astype(jnp.float32) * scale
        qpos = bi * Q_BLOCK + jnp.arange(Q_BLOCK)
        past = kpos[None, :] < qpos[:, None]
        log_keep = jnp.where(past, jax.nn.log_sigmoid(-z), 0.0)
        later = lax.cumsum(log_keep, axis=3, reverse=True) - log_keep
        weights = jnp.where(past, jnp.exp(jax.nn.log_sigmoid(z) + later), 0.0)
        return jnp.einsum('bhqk,bhkd->bhqd', weights.astype(v.dtype), v)

    out = lax.map(block, (_to_blocks(q), jnp.arange(seq // Q_BLOCK)))
    return _from_blocks(out)


def _diff_attention(q, k, v, lam):
    seq = k.shape[-2]
    scale = q.shape[-1] ** -0.5
    kpos = jnp.arange(seq)

    def block(args):
        qi, bi = args
        s = jnp.einsum('bhcqd,bhckd->bhcqk', qi, k).astype(jnp.float32) * scale
        qpos = bi * Q_BLOCK + jnp.arange(Q_BLOCK)
        causal = kpos[None, :] <= qpos[:, None]
        pr = jax.nn.softmax(jnp.where(causal, s, -jnp.inf), axis=-1)
        attn = pr[:, :, 0] - lam * pr[:, :, 1]
        return jnp.einsum('bhqk,bhkv->bhqv', attn.astype(v.dtype), v)

    out = lax.map(block, (_to_blocks(q), jnp.arange(seq // Q_BLOCK)))
    return _from_blocks(out)


def setup_inputs(seed: int = 0) -> dict:
    key = jax.random.key(seed)
    ks = jax.random.split(key, 20)

    def nrm(k, shape, scale):
        return jax.random.normal(k, shape, jnp.float32) * scale

    return {
        'x': nrm(ks[0], (BATCH, SEQ, D_MODEL), 1.0),
        'p': nrm(ks[1], (DEPTH, BATCH, SEQ, PLE_DIM), 1.0),
        'norm_w': 1.0 + nrm(ks[2], (DEPTH, D_MODEL), 0.05),
        'w_in': nrm(ks[3], (DEPTH, D_MODEL, IN_W), D_MODEL ** -0.5),
        'b_gate': nrm(ks[4], (DEPTH, N_BRANCH * D_MODEL), 0.01),
        'conv_w': nrm(ks[5], (DEPTH, CONV_WIDTH, CONV_DIM), CONV_WIDTH ** -0.5),
        'diff_q_norm': 1.0 + nrm(ks[6], (DEPTH, DIFF_HEAD_DIM), 0.05),
        'diff_k_norm': 1.0 + nrm(ks[7], (DEPTH, DIFF_HEAD_DIM), 0.05),
        'lam_q1': nrm(ks[8], (DEPTH, DIFF_HEAD_DIM), 0.1),
        'lam_k1': nrm(ks[9], (DEPTH, DIFF_HEAD_DIM), 0.1),
        'lam_q2': nrm(ks[10], (DEPTH, DIFF_HEAD_DIM), 0.1),
        'lam_k2': nrm(ks[11], (DEPTH, DIFF_HEAD_DIM), 0.1),
        'diff_subln': 1.0 + nrm(ks[12], (DEPTH, DIFF_V_DIM), 0.05),
        'w_out_conv': nrm(ks[13], (DEPTH, CONV_DIM, D_MODEL), CONV_DIM ** -0.5),
        'w_out_sb': nrm(ks[14], (DEPTH, SB_HEADS * SB_HEAD_DIM, D_MODEL), (SB_HEADS * SB_HEAD_DIM) ** -0.5),
        'w_out_diff': nrm(ks[15], (DEPTH, DIFF_HEADS * DIFF_V_DIM, D_MODEL), (DIFF_HEADS * DIFF_V_DIM) ** -0.5),
        'w_out': nrm(ks[16], (DEPTH, D_MODEL, D_MODEL), D_MODEL ** -0.5),
        'ple_norm': 1.0 + nrm(ks[17], (DEPTH, D_MODEL), 0.05),
        'w_ple_gate': nrm(ks[18], (DEPTH, D_MODEL, D_MODEL), D_MODEL ** -0.5),
        'w_ple': nrm(ks[19], (DEPTH, PLE_DIM, D_MODEL), PLE_DIM ** -0.5),
    }


def reference(x, p, norm_w, w_in, b_gate, conv_w, diff_q_norm, diff_k_norm,
              lam_q1, lam_k1, lam_q2, lam_k2, diff_subln, w_out_conv, w_out_sb,
              w_out_diff, w_out, ple_norm, w_ple_gate, w_ple):
    b, s, _ = x.shape
    cos, sin = _rope_tables(s, DIFF_HEAD_DIM, x.dtype)
    pts = _split_points()
    for i in range(DEPTH):
        xn = _rmsnorm(x, norm_w[i])
        proj = xn @ w_in[i]
        (c_b, c_c, c_u, c_z, s_q, s_k, s_v, s_z,
         d_q, d_k, d_v, d_z, g_logits) = jnp.split(proj, pts, axis=-1)

        y_a = c_b * _causal_conv(c_c * c_u, conv_w[i])
        y_a = (y_a * jax.nn.silu(c_z)) @ w_out_conv[i]

        def heads_sb(t):
            return t.reshape(b, s, SB_HEADS, SB_HEAD_DIM).transpose(0, 2, 1, 3)
        o_b = _stick_breaking_attention(heads_sb(s_q), heads_sb(s_k), heads_sb(s_v))
        o_b = o_b.transpose(0, 2, 1, 3).reshape(b, s, SB_HEADS * SB_HEAD_DIM)
        y_b = (o_b * jax.nn.silu(s_z)) @ w_out_sb[i]

        lam_init = 0.8 - 0.6 * math.exp(-0.3 * i)
        lam = (jnp.exp(jnp.sum(lam_q1[i].astype(jnp.float32) * lam_k1[i].astype(jnp.float32)))
               - jnp.exp(jnp.sum(lam_q2[i].astype(jnp.float32) * lam_k2[i].astype(jnp.float32)))
               + lam_init)
        qd = d_q.reshape(b, s, DIFF_HEADS, 2, DIFF_HEAD_DIM).transpose(0, 2, 3, 1, 4)
        kd = d_k.reshape(b, s, DIFF_HEADS, 2, DIFF_HEAD_DIM).transpose(0, 2, 3, 1, 4)
        qd = _rope(_rmsnorm(qd, diff_q_norm[i]), cos, sin)
        kd = _rope(_rmsnorm(kd, diff_k_norm[i]), cos, sin)
        vd = d_v.reshape(b, s, DIFF_HEADS, DIFF_V_DIM).transpose(0, 2, 1, 3)
        o_c = _diff_attention(qd, kd, vd, lam)
        o_c = _rmsnorm(o_c, diff_subln[i]) * (1.0 - lam_init)
        o_c = o_c.transpose(0, 2, 1, 3).reshape(b, s, DIFF_HEADS * DIFF_V_DIM)
        y_c = (o_c * jax.nn.silu(d_z)) @ w_out_diff[i]

        g = jax.nn.sigmoid(g_logits + b_gate[i]).reshape(b, s, N_BRANCH, D_MODEL)
        h = g[:, :, 0] * y_a + g[:, :, 1] * y_b + g[:, :, 2] * y_c
        x = x + h @ w_out[i]

        pg = jax.nn.sigmoid(_rmsnorm(x, ple_norm[i]) @ w_ple_gate[i])
        x = x + pg * (p[i] @ w_ple[i])
    return x
```

```python
import functools
import math

import jax
import jax.numpy as jnp
from jax import lax
from jax.experimental import pallas as pl
from jax.experimental.pallas import tpu as pltpu

F32 = jnp.float32
BF16 = jnp.bfloat16

D_MODEL = 1024
PLE_DIM = 256
N_BRANCH = 3
CONV_DIM = 512
CONV_WIDTH = 3
SB_HEADS = 8
SB_HEAD_DIM = 64
DIFF_HEADS = 4
DIFF_HEAD_DIM = 64
DIFF_V_DIM = 2 * DIFF_HEAD_DIM
ROPE_THETA = 10000.0
EPS = 1e-6

BRANCH_W = 512
LANES = 128
SUBLANES = 8
MXU_DIM = 256
VMEM_LIMIT = 56 * 1024 * 1024

_OFF_CONV = 0
_OFF_SQ = 4 * CONV_DIM
_OFF_SZ = _OFF_SQ + 3 * BRANCH_W
_OFF_DQ = _OFF_SZ + BRANCH_W
_OFF_DZ = _OFF_DQ + 3 * BRANCH_W
_OFF_G = _OFF_DZ + BRANCH_W


def _tiles(seq):
    tm_pre = min(512, seq)
    tm_post = min(256, seq)
    q_blk = min(MXU_DIM, seq)
    assert seq % tm_pre == 0 and seq % tm_post == 0 and seq % q_blk == 0
    return tm_pre, tm_post, q_blk


def _split_bf16(a):
    hi = a.astype(BF16)
    lo = (a - hi.astype(F32)).astype(BF16)
    return hi, lo


def _rms_scale(x, w):
    ms = jnp.mean(x * x, axis=-1, keepdims=True)
    return x * lax.rsqrt(ms + EPS) * w


def _sigmoid(a):
    return 1.0 / (1.0 + jnp.exp(-a))


def _silu(a):
    return a * _sigmoid(a)


def _attn_in_proj_kernel(x_ref, nw_ref, w_ref, cos_ref, sin_ref, qn_ref, kn_ref, seg_ref,
                         sq_ref, sk_ref, sv_ref, dq_ref, dk_ref, dv_ref):
    xn = _rms_scale(x_ref[...], nw_ref[...]).astype(BF16)

    def proj(c):
        return jnp.dot(xn, w_ref[:, c * BRANCH_W:(c + 1) * BRANCH_W],
                       preferred_element_type=F32)

    sq_ref[...] = (proj(0) * (SB_HEAD_DIM ** -0.5)).astype(BF16)
    sk_ref[...] = proj(1).astype(BF16)
    sv_ref[...] = proj(2).astype(BF16)

    reps = BRANCH_W // LANES
    cos = jnp.concatenate([cos_ref[...]] * reps, axis=1)
    sin = jnp.concatenate([sin_ref[...]] * reps, axis=1)
    lane = lax.broadcasted_iota(jnp.int32, cos.shape, 1)
    first_half = (lane & (DIFF_HEAD_DIM - 1)) < (DIFF_HEAD_DIM // 2)
    seg = seg_ref[...]

    def norm_rope(y, wn):
        hi, lo = _split_bf16(y * y)
        parts = []
        for c in range(BRANCH_W // MXU_DIM):
            sl = slice(c * MXU_DIM, (c + 1) * MXU_DIM)
            parts.append(jnp.dot(hi[:, sl], seg, preferred_element_type=F32)
                         + jnp.dot(lo[:, sl], seg, preferred_element_type=F32))
        ms = jnp.concatenate(parts, axis=1) * (1.0 / DIFF_HEAD_DIM)
        yn = y * lax.rsqrt(ms + EPS) * wn
        half = DIFF_HEAD_DIM // 2
        partner = jnp.where(first_half,
                            pltpu.roll(yn, BRANCH_W - half, 1),
                            pltpu.roll(yn, half, 1))
        return yn * cos + partner * sin

    dq_ref[...] = (norm_rope(proj(3), qn_ref[...]) * (DIFF_HEAD_DIM ** -0.5)).astype(BF16)
    dk_ref[...] = norm_rope(proj(4), kn_ref[...]).astype(BF16)
    dv_ref[...] = proj(5).astype(BF16)


def _attn_in_proj(x2d, norm_w, w_attn, cos_t, sin_t, qn, kn, seg, seq, tm):
    m = x2d.shape[0]
    n_seq_tiles = seq // tm
    const = lambda i: (0, 0)
    out = jax.ShapeDtypeStruct((m, BRANCH_W), BF16)
    out_spec = pl.BlockSpec((tm, BRANCH_W), lambda i: (i, 0))
    return pl.pallas_call(
        _attn_in_proj_kernel,
        grid=(m // tm,),
        in_specs=[
            pl.BlockSpec((tm, D_MODEL), lambda i: (i, 0)),
            pl.BlockSpec((1, D_MODEL), const),
            pl.BlockSpec((D_MODEL, 6 * BRANCH_W), const),
            pl.BlockSpec((tm, LANES), lambda i: (i % n_seq_tiles, 0)),
            pl.BlockSpec((tm, LANES), lambda i: (i % n_seq_tiles, 0)),
            pl.BlockSpec((1, BRANCH_W), const),
            pl.BlockSpec((1, BRANCH_W), const),
            pl.BlockSpec((MXU_DIM, MXU_DIM), const),
        ],
        out_specs=[out_spec] * 6,
        out_shape=[out] * 6,
        compiler_params=pltpu.CompilerParams(
            dimension_semantics=("arbitrary",), vmem_limit_bytes=VMEM_LIMIT),
        name="attn_in_proj",
    )(x2d, norm_w, w_attn, cos_t, sin_t, qn, kn, seg)


def _sb_attn_kernel(q_ref, k_ref, v_ref, tri_ref, o_ref, *, blk):
    qi = pl.program_id(2)
    qf = q_ref[...].astype(F32)
    lane = lax.broadcasted_iota(jnp.int32, qf.shape, 1)
    first = lane < SB_HEAD_DIM
    q_heads = [jnp.where(first, qf, 0.0).astype(BF16), jnp.where(first, 0.0, qf).astype(BF16)]
    tri = tri_ref[...]
    row = lax.broadcasted_iota(jnp.int32, (blk, blk), 0)
    col = lax.broadcasted_iota(jnp.int32, (blk, blk), 1)
    strictly_past = col < row

    def tile(qh, kblk, vblk, run, acc, mask):
        z = lax.dot_general(qh, kblk, (((1,), (1,)), ((), ())), preferred_element_type=F32)
        sp = jnp.maximum(z, 0.0) + jnp.log(1.0 + jnp.exp(-jnp.abs(z)))
        if mask is not None:
            sp = jnp.where(mask, sp, 0.0)
        hi, lo = _split_bf16(sp)
        c = (jnp.dot(hi, tri, preferred_element_type=F32)
             + jnp.dot(lo, tri, preferred_element_type=F32) + run)
        w = jnp.exp(z - c)
        if mask is not None:
            w = jnp.where(mask, w, 0.0)
        acc = acc + jnp.dot(w.astype(BF16), vblk, preferred_element_type=F32)
        return c[:, 0:1], acc

    def step(k0, carry, mask):
        kblk = k_ref[pl.ds(k0, blk), :]
        vblk = v_ref[pl.ds(k0, blk), :]
        out = []
        for h in range(2):
            out.extend(tile(q_heads[h], kblk, vblk, carry[2 * h], carry[2 * h + 1], mask))
        return tuple(out)

    zero_run = jnp.zeros((blk, 1), F32)
    zero_acc = jnp.zeros((blk, LANES), F32)
    carry = step(pl.multiple_of(qi * blk, blk), (zero_run, zero_acc, zero_run, zero_acc),
                 strictly_past)

    def body(j, carry):
        return step(pl.multiple_of((qi - 1 - j) * blk, blk), carry, None)

    carry = lax.fori_loop(0, qi, body, carry)
    o_ref[...] = jnp.where(lane < SB_HEAD_DIM, carry[1], carry[3])


def _sb_attn(q, k, v, tri, blk):
    b, seq, _ = q.shape
    n_pairs = BRANCH_W // LANES
    kv_spec = pl.BlockSpec((None, seq, LANES), lambda bi, hp, qi: (bi, 0, hp))
    return pl.pallas_call(
        functools.partial(_sb_attn_kernel, blk=blk),
        grid=(b, n_pairs, seq // blk),
        in_specs=[
            pl.BlockSpec((None, blk, LANES), lambda bi, hp, qi: (bi, qi, hp)),
            kv_spec, kv_spec,
            pl.BlockSpec((blk, blk), lambda bi, hp, qi: (0, 0)),
        ],
        out_specs=pl.BlockSpec((None, blk, LANES), lambda bi, hp, qi: (bi, qi, hp)),
        out_shape=jax.ShapeDtypeStruct((b, seq, BRANCH_W), F32),
        compiler_params=pltpu.CompilerParams(
            dimension_semantics=("arbitrary", "arbitrary", "arbitrary"),
            vmem_limit_bytes=VMEM_LIMIT),
        name="sb_attn",
    )(q, k, v, tri)


def _diff_attn_kernel(lq1_ref, lk1_ref, lq2_ref, lk2_ref, q_ref, k_ref, v_ref, o_ref,
                      *, blk, lam_init):
    qi = pl.program_id(2)
    qf = q_ref[...].astype(F32)
    lane = lax.broadcasted_iota(jnp.int32, qf.shape, 1)
    first = lane < DIFF_HEAD_DIM
    q_maps = [jnp.where(first, qf, 0.0).astype(BF16), jnp.where(first, 0.0, qf).astype(BF16)]
    row = lax.broadcasted_iota(jnp.int32, (blk, blk), 0)
    col = lax.broadcasted_iota(jnp.int32, (blk, blk), 1)
    causal = col <= row

    def tile(qc, kblk, vblk, m, l, acc, mask):
        s = lax.dot_general(qc, kblk, (((1,), (1,)), ((), ())), preferred_element_type=F32)
        if mask is not None:
            s = jnp.where(mask, s, -jnp.inf)
        m_new = jnp.maximum(m, jnp.max(s, axis=-1, keepdims=True))
        alpha = jnp.exp(m - m_new)
        p = jnp.exp(s - m_new)
        l = alpha * l + jnp.sum(p, axis=-1, keepdims=True)
        acc = alpha * acc + jnp.dot(p.astype(BF16), vblk, preferred_element_type=F32)
        return m_new, l, acc

    def step(k0, carry, mask):
        kblk = k_ref[pl.ds(k0, blk), :]
        vblk = v_ref[pl.ds(k0, blk), :]
        out = []
        for c in range(2):
            out.extend(tile(q_maps[c], kblk, vblk, *carry[3 * c:3 * c + 3], mask))
        return tuple(out)

    init = (jnp.full((blk, 1), -jnp.inf, F32), jnp.zeros((blk, 1), F32),
            jnp.zeros((blk, LANES), F32))
    carry = step(pl.multiple_of(qi * blk, blk), init + init, causal)

    def body(j, carry):
        return step(pl.multiple_of((qi - 1 - j) * blk, blk), carry, None)

    carry = lax.fori_loop(0, qi, body, carry)
    lam = (jnp.exp(jnp.sum(lq1_ref[...] * lk1_ref[...], axis=-1, keepdims=True))
           - jnp.exp(jnp.sum(lq2_ref[...] * lk2_ref[...], axis=-1, keepdims=True))
           + lam_init)
    o_ref[...] = carry[2] / carry[1] - lam * (carry[5] / carry[4])


def _diff_attn(lam_vecs, q, k, v, blk, lam_init):
    b, seq, _ = q.shape
    lam_spec = pl.BlockSpec((1, DIFF_HEAD_DIM), lambda bi, h, qi: (0, 0))
    kv_spec = pl.BlockSpec((None, seq, LANES), lambda bi, h, qi: (bi, 0, h))
    return pl.pallas_call(
        functools.partial(_diff_attn_kernel, blk=blk, lam_init=lam_init),
        grid=(b, DIFF_HEADS, seq // blk),
        in_specs=[lam_spec] * 4 + [
            pl.BlockSpec((None, blk, LANES), lambda bi, h, qi: (bi, qi, h)),
            kv_spec, kv_spec,
        ],
        out_specs=pl.BlockSpec((None, blk, LANES), lambda bi, h, qi: (bi, qi, h)),
        out_shape=jax.ShapeDtypeStruct((b, seq, BRANCH_W), F32),
        compiler_params=pltpu.CompilerParams(
            dimension_semantics=("arbitrary", "arbitrary", "arbitrary"),
            vmem_limit_bytes=VMEM_LIMIT),
        name="diff_attn",
    )(*lam_vecs, q, k, v)


def _merge_out_kernel(x_ref, ob_ref, oc_ref, p_ref, nw_ref, wc_ref, wz_ref, wg_ref, bg_ref,
                      cw_ref, sub_ref, woc_ref, wos_ref, wod_ref, wo_ref, pn_ref, wpg_ref,
                      wp_ref, out_ref, tail_ref, *, tiles_per_seq, lam_init):
    tm = x_ref.shape[0]

    @pl.when(pl.program_id(0) % tiles_per_seq == 0)
    def _():
        tail_ref[...] = jnp.zeros_like(tail_ref)

    x = x_ref[...]
    xn = _rms_scale(x, nw_ref[...]).astype(BF16)

    def mm(a, w):
        return jnp.dot(a, w, preferred_element_type=F32)

    c_b = mm(xn, wc_ref[:, 0 * CONV_DIM:1 * CONV_DIM])
    u = mm(xn, wc_ref[:, 1 * CONV_DIM:2 * CONV_DIM]) * mm(xn, wc_ref[:, 2 * CONV_DIM:3 * CONV_DIM])
    ext = jnp.concatenate([tail_ref[...], u], axis=0)
    tail_ref[...] = u[tm - SUBLANES:, :]
    conv = cw_ref[CONV_WIDTH - 1:CONV_WIDTH, :] * u
    for j in range(1, CONV_WIDTH):
        conv = conv + (cw_ref[CONV_WIDTH - 1 - j:CONV_WIDTH - j, :]
                       * pltpu.roll(ext, j, 0)[SUBLANES:, :])
    c_z = mm(xn, wc_ref[:, 3 * CONV_DIM:4 * CONV_DIM])
    y_a = mm((c_b * conv * _silu(c_z)).astype(BF16), woc_ref[...])

    s_z = mm(xn, wz_ref[:, 0:BRANCH_W])
    y_b = mm((ob_ref[...] * _silu(s_z)).astype(BF16), wos_ref[...])

    oc = oc_ref[...]
    sub = sub_ref[...]
    heads = []
    for h in range(DIFF_HEADS):
        heads.append(_rms_scale(oc[:, h * DIFF_V_DIM:(h + 1) * DIFF_V_DIM], sub) * (1.0 - lam_init))
    oc_n = jnp.concatenate(heads, axis=1)
    d_z = mm(xn, wz_ref[:, BRANCH_W:2 * BRANCH_W])
    y_c = mm((oc_n * _silu(d_z)).astype(BF16), wod_ref[...])

    h_mix = None
    for n, y in enumerate((y_a, y_b, y_c)):
        g = _sigmoid(mm(xn, wg_ref[:, n * D_MODEL:(n + 1) * D_MODEL])
                     + bg_ref[:, n * D_MODEL:(n + 1) * D_MODEL])
        h_mix = g * y if h_mix is None else h_mix + g * y
    x1 = x + mm(h_mix.astype(BF16), wo_ref[...])

    pg = _sigmoid(mm(_rms_scale(x1, pn_ref[...]).astype(BF16), wpg_ref[...]))
    out_ref[...] = x1 + pg * mm(p_ref[...].astype(BF16), wp_ref[...])


def _merge_out(x2d, ob, oc, p2d, norm_w, w_conv, w_z, w_g, b_gate, conv_w, subln, w_out_conv,
               w_out_sb, w_out_diff, w_out, ple_norm, w_ple_gate, w_ple, seq, tm, lam_init):
    m = x2d.shape[0]
    const = lambda i: (0, 0)

    def resident(a):
        return pl.BlockSpec(a.shape, const, pipeline_mode=pl.Buffered(1))

    def rows(width):
        return pl.BlockSpec((tm, width), lambda i: (i, 0))

    weights = (norm_w, w_conv, w_z, w_g, b_gate, conv_w, subln, w_out_conv, w_out_sb,
               w_out_diff, w_out, ple_norm, w_ple_gate, w_ple)
    return pl.pallas_call(
        functools.partial(_merge_out_kernel, tiles_per_seq=seq // tm, lam_init=lam_init),
        grid=(m // tm,),
        in_specs=[rows(D_MODEL), rows(BRANCH_W), rows(BRANCH_W), rows(PLE_DIM)]
                 + [resident(a) for a in weights],
        out_specs=rows(D_MODEL),
        out_shape=jax.ShapeDtypeStruct((m, D_MODEL), F32),
        scratch_shapes=[pltpu.VMEM((SUBLANES, CONV_DIM), F32)],
        compiler_params=pltpu.CompilerParams(
            dimension_semantics=("arbitrary",), vmem_limit_bytes=VMEM_LIMIT),
        name="merge_out",
    )(x2d, ob, oc, p2d, *weights)


def _rope_tables(seq):
    pos = jnp.arange(seq, dtype=F32)
    inv = 1.0 / (ROPE_THETA ** (jnp.arange(0, DIFF_HEAD_DIM, 2, dtype=F32) / DIFF_HEAD_DIM))
    ang = pos[:, None] * inv[None, :]
    cos, sin = jnp.cos(ang), jnp.sin(ang)
    reps = LANES // DIFF_HEAD_DIM
    cos_t = jnp.concatenate([cos, cos] * reps, axis=1)
    sin_t = jnp.concatenate([-sin, sin] * reps, axis=1)
    return cos_t, sin_t


def kernel(x, p, norm_w, w_in, b_gate, conv_w, diff_q_norm, diff_k_norm, lam_q1, lam_k1, lam_q2,
           lam_k2, diff_subln, w_out_conv, w_out_sb, w_out_diff, w_out, ple_norm, w_ple_gate,
           w_ple):
    b, seq, _ = x.shape
    depth = w_in.shape[0]
    tm_pre, tm_post, q_blk = _tiles(seq)
    cos_t, sin_t = _rope_tables(seq)
    idx = jnp.arange(MXU_DIM)
    seg = (idx[:, None] // DIFF_HEAD_DIM == idx[None, :] // DIFF_HEAD_DIM).astype(BF16)
    kidx = jnp.arange(q_blk)
    tri = (kidx[:, None] >= kidx[None, :]).astype(BF16)
    row = lambda a: a.reshape(1, -1)

    x2d = x.reshape(b * seq, D_MODEL)
    for i in range(depth):
        lam_init = 0.8 - 0.6 * math.exp(-0.3 * i)
        w = w_in[i].astype(BF16)
        w_attn = jnp.concatenate([w[:, _OFF_SQ:_OFF_SZ], w[:, _OFF_DQ:_OFF_DZ]], axis=1)
        w_z = jnp.concatenate([w[:, _OFF_SZ:_OFF_DQ], w[:, _OFF_DZ:_OFF_G]], axis=1)
        n_maps = BRANCH_W // DIFF_HEAD_DIM

        sq, sk, sv, dq, dk, dv = _attn_in_proj(
            x2d, row(norm_w[i]), w_attn, cos_t, sin_t,
            row(jnp.tile(diff_q_norm[i], n_maps)), row(jnp.tile(diff_k_norm[i], n_maps)),
            seg, seq, tm_pre)
        shp = (b, seq, BRANCH_W)
        ob = _sb_attn(sq.reshape(shp), sk.reshape(shp), sv.reshape(shp), tri, q_blk)
        oc = _diff_attn((row(lam_q1[i]), row(lam_k1[i]), row(lam_q2[i]), row(lam_k2[i])),
                        dq.reshape(shp), dk.reshape(shp), dv.reshape(shp), q_blk, lam_init)
        x2d = _merge_out(
            x2d, ob.reshape(b * seq, BRANCH_W), oc.reshape(b * seq, BRANCH_W),
            p[i].reshape(b * seq, PLE_DIM), row(norm_w[i]), w[:, _OFF_CONV:_OFF_SQ], w_z,
            w[:, _OFF_G:], row(b_gate[i]), conv_w[i], row(diff_subln[i]),
            w_out_conv[i].astype(BF16), w_out_sb[i].astype(BF16), w_out_diff[i].astype(BF16),
            w_out[i].astype(BF16), row(ple_norm[i]), w_ple_gate[i].astype(BF16),
            w_ple[i].astype(BF16), seq, tm_post, lam_init)
    return x2d.reshape(b, seq, D_MODEL)
```

```python
import functools
import math

import jax
import jax.numpy as jnp
from jax import lax
from jax.experimental import pallas as pl
from jax.experimental.pallas import tpu as pltpu

F32 = jnp.float32
BF16 = jnp.bfloat16

D_MODEL = 1024
PLE_DIM = 256
N_BRANCH = 3
CONV_DIM = 512
CONV_WIDTH = 3
SB_HEADS = 8
SB_HEAD_DIM = 64
DIFF_HEADS = 4
DIFF_HEAD_DIM = 64
DIFF_V_DIM = 2 * DIFF_HEAD_DIM
ROPE_THETA = 10000.0
EPS = 1e-6

BRANCH_W = 512
LANES = 128
SUBLANES = 8
MXU_DIM = 256
VMEM_LIMIT = 56 * 1024 * 1024
_MASKED = -1e30

_OFF_CONV = 0
_OFF_SQ = 4 * CONV_DIM
_OFF_SZ = _OFF_SQ + 3 * BRANCH_W
_OFF_DQ = _OFF_SZ + BRANCH_W
_OFF_DZ = _OFF_DQ + 3 * BRANCH_W
_OFF_G = _OFF_DZ + BRANCH_W


def _tiles(seq):
    tm_pre = min(512, seq)
    tm_post = min(256, seq)
    q_blk = min(MXU_DIM, seq)
    assert seq % tm_pre == 0 and seq % tm_post == 0 and seq % q_blk == 0
    return tm_pre, tm_post, q_blk


def _split_bf16(a):
    hi = a.astype(BF16)
    lo = (a - hi.astype(F32)).astype(BF16)
    return hi, lo


def _rms_scale(x, w):
    ms = jnp.mean(x * x, axis=-1, keepdims=True)
    return x * lax.rsqrt(ms + EPS) * w


def _sigmoid(a):
    return 1.0 / (1.0 + jnp.exp(-a))


def _silu(a):
    return a * _sigmoid(a)


def _attn_in_proj_kernel(x_ref, nw_ref, w_ref, cos_ref, sin_ref, qn_ref, kn_ref, seg_ref,
                         sq_ref, sk_ref, sv_ref, dq_ref, dk_ref, dv_ref):
    xn = _rms_scale(x_ref[...], nw_ref[...]).astype(BF16)

    def proj(c):
        return jnp.dot(xn, w_ref[:, c * BRANCH_W:(c + 1) * BRANCH_W],
                       preferred_element_type=F32)

    sq_ref[...] = (proj(0) * (SB_HEAD_DIM ** -0.5)).astype(BF16)
    sk_ref[...] = proj(1).astype(BF16)
    sv_ref[...] = proj(2).astype(BF16)

    reps = BRANCH_W // LANES
    cos = jnp.concatenate([cos_ref[...]] * reps, axis=1)
    sin = jnp.concatenate([sin_ref[...]] * reps, axis=1)
    lane = lax.broadcasted_iota(jnp.int32, cos.shape, 1)
    first_half = (lane & (DIFF_HEAD_DIM - 1)) < (DIFF_HEAD_DIM // 2)
    seg = seg_ref[...]

    def norm_rope(y, wn):
        hi, lo = _split_bf16(y * y)
        parts = []
        for c in range(BRANCH_W // MXU_DIM):
            sl = slice(c * MXU_DIM, (c + 1) * MXU_DIM)
            parts.append(jnp.dot(hi[:, sl], seg, preferred_element_type=F32)
                         + jnp.dot(lo[:, sl], seg, preferred_element_type=F32))
        ms = jnp.concatenate(parts, axis=1) * (1.0 / DIFF_HEAD_DIM)
        yn = y * lax.rsqrt(ms + EPS) * wn
        half = DIFF_HEAD_DIM // 2
        partner = jnp.where(first_half,
                            pltpu.roll(yn, BRANCH_W - half, 1),
                            pltpu.roll(yn, half, 1))
        return yn * cos + partner * sin

    dq_ref[...] = (norm_rope(proj(3), qn_ref[...]) * (DIFF_HEAD_DIM ** -0.5)).astype(BF16)
    dk_ref[...] = norm_rope(proj(4), kn_ref[...]).astype(BF16)
    dv_ref[...] = proj(5).astype(BF16)


def _attn_in_proj(x2d, norm_w, w_attn, cos_t, sin_t, qn, kn, seg, seq, tm):
    m = x2d.shape[0]
    n_seq_tiles = seq // tm
    const = lambda i: (0, 0)
    out = jax.ShapeDtypeStruct((m, BRANCH_W), BF16)
    out_spec = pl.BlockSpec((tm, BRANCH_W), lambda i: (i, 0))
    return pl.pallas_call(
        _attn_in_proj_kernel,
        grid=(m // tm,),
        in_specs=[
            pl.BlockSpec((tm, D_MODEL), lambda i: (i, 0)),
            pl.BlockSpec((1, D_MODEL), const),
            pl.BlockSpec((D_MODEL, 6 * BRANCH_W), const),
            pl.BlockSpec((tm, LANES), lambda i: (i % n_seq_tiles, 0)),
            pl.BlockSpec((tm, LANES), lambda i: (i % n_seq_tiles, 0)),
            pl.BlockSpec((1, BRANCH_W), const),
            pl.BlockSpec((1, BRANCH_W), const),
            pl.BlockSpec((MXU_DIM, MXU_DIM), const),
        ],
        out_specs=[out_spec] * 6,
        out_shape=[out] * 6,
        compiler_params=pltpu.CompilerParams(
            dimension_semantics=("arbitrary",), vmem_limit_bytes=VMEM_LIMIT),
        name="attn_in_proj",
    )(x2d, norm_w, w_attn, cos_t, sin_t, qn, kn, seg)


def _sb_attn_kernel(q_ref, k_ref, v_ref, tri_ref, o_ref,
                    z_scr, d_scr, tot_scr, run_scr, acc_scr, *, blk):
    qi = pl.program_id(2)
    qf = q_ref[...].astype(F32)
    lane = lax.broadcasted_iota(jnp.int32, qf.shape, 1)
    first = lane < SB_HEAD_DIM
    q_heads = [jnp.where(first, qf, 0.0).astype(BF16), jnp.where(first, 0.0, qf).astype(BF16)]
    tri = tri_ref[...]
    row = lax.broadcasted_iota(jnp.int32, (blk, blk), 0)
    col = lax.broadcasted_iota(jnp.int32, (blk, blk), 1)
    strictly_past = col < row


    def key_start(t):
        return pl.multiple_of((qi - jnp.maximum(t, 0)) * blk, blk)

    def logits(t, mask=None):
        kblk = k_ref[pl.ds(key_start(t), blk), :]
        for h in range(2):
            z = lax.dot_general(q_heads[h], kblk, (((1,), (1,)), ((), ())),
                                preferred_element_type=F32)
            z_scr[h] = z if mask is None else jnp.where(mask, z, _MASKED)

    def minus_local_cumsum():
        for h in range(2):
            z = z_scr[h]
            sp = jnp.maximum(z, 0.0) + jnp.log(1.0 + jnp.exp(-jnp.abs(z)))
            hi, lo = _split_bf16(sp)
            c = (jnp.dot(hi, tri, preferred_element_type=F32)
                 + jnp.dot(lo, tri, preferred_element_type=F32))
            d_scr[h] = z - c
            tot_scr[h] = jnp.broadcast_to(c[:, 0:1], (blk, LANES))

    def accumulate(t):
        vblk = v_ref[pl.ds(key_start(t), blk), :]
        for h in range(2):
            run = run_scr[h]
            w = jnp.exp(d_scr[h] - jnp.concatenate([run] * (blk // LANES), axis=1))
            acc_scr[h] += jnp.dot(w.astype(BF16), vblk, preferred_element_type=F32)
            run_scr[h] = run + tot_scr[h]

    d_scr[...] = jnp.full(d_scr.shape, _MASKED, F32)
    tot_scr[...] = jnp.zeros(tot_scr.shape, F32)
    run_scr[...] = jnp.zeros(run_scr.shape, F32)
    acc_scr[...] = jnp.zeros(acc_scr.shape, F32)
    logits(0, strictly_past)

    @pl.loop(1, qi + 1)
    def _(i):
        accumulate(i - 2)
        minus_local_cumsum()
        logits(i)

    accumulate(qi - 1)
    minus_local_cumsum()
    accumulate(qi)
    o_ref[...] = jnp.where(first, acc_scr[0], acc_scr[1])


def _sb_attn(q, k, v, tri, blk):
    b, seq, _ = q.shape
    n_pairs = BRANCH_W // LANES
    kv_spec = pl.BlockSpec((None, seq, LANES), lambda bi, hp, qi: (bi, 0, hp))
    tile_scr = pltpu.VMEM((2, blk, blk), F32)
    row_scr = pltpu.VMEM((2, blk, LANES), F32)
    return pl.pallas_call(
        functools.partial(_sb_attn_kernel, blk=blk),
        grid=(b, n_pairs, seq // blk),
        in_specs=[
            pl.BlockSpec((None, blk, LANES), lambda bi, hp, qi: (bi, qi, hp)),
            kv_spec, kv_spec,
            pl.BlockSpec((blk, blk), lambda bi, hp, qi: (0, 0)),
        ],
        out_specs=pl.BlockSpec((None, blk, LANES), lambda bi, hp, qi: (bi, qi, hp)),
        out_shape=jax.ShapeDtypeStruct((b, seq, BRANCH_W), F32),
        scratch_shapes=[tile_scr, tile_scr, row_scr, row_scr, row_scr],
        compiler_params=pltpu.CompilerParams(
            dimension_semantics=("arbitrary", "arbitrary", "arbitrary"),
            vmem_limit_bytes=VMEM_LIMIT),
        name="sb_attn",
    )(q, k, v, tri)


def _diff_attn_kernel(lq1_ref, lk1_ref, lq2_ref, lk2_ref, q_ref, k_ref, v_ref, o_ref,
                      s_scr, p_scr, alpha_scr, m_scr, l_scr, acc_scr, *, blk, lam_init):
    qi = pl.program_id(2)
    qf = q_ref[...].astype(F32)
    lane = lax.broadcasted_iota(jnp.int32, qf.shape, 1)
    first = lane < DIFF_HEAD_DIM
    q_maps = [jnp.where(first, qf, 0.0).astype(BF16), jnp.where(first, 0.0, qf).astype(BF16)]
    row = lax.broadcasted_iota(jnp.int32, (blk, blk), 0)
    col = lax.broadcasted_iota(jnp.int32, (blk, blk), 1)
    causal = col <= row


    def key_start(t):
        return pl.multiple_of((qi - jnp.maximum(t, 0)) * blk, blk)

    def scores(t, mask=None):
        kblk = k_ref[pl.ds(key_start(t), blk), :]
        for c in range(2):
            s = lax.dot_general(q_maps[c], kblk, (((1,), (1,)), ((), ())),
                                preferred_element_type=F32)
            s_scr[c] = s if mask is None else jnp.where(mask, s, -jnp.inf)

    def row_stat(a):
        return jnp.broadcast_to(a, (blk, LANES))

    def softmax_update():
        for c in range(2):
            s = s_scr[c]
            m = m_scr[c]
            m_new = jnp.maximum(m, row_stat(jnp.max(s, axis=-1, keepdims=True)))
            alpha = jnp.exp(m - m_new)
            p = jnp.exp(s - jnp.concatenate([m_new] * (blk // LANES), axis=1))
            p_scr[c] = p.astype(BF16)
            alpha_scr[c] = alpha
            m_scr[c] = m_new
            l_scr[c] = alpha * l_scr[c] + row_stat(jnp.sum(p, axis=-1, keepdims=True))

    def accumulate(t):
        vblk = v_ref[pl.ds(key_start(t), blk), :]
        for c in range(2):
            acc_scr[c] = (alpha_scr[c] * acc_scr[c]
                          + jnp.dot(p_scr[c], vblk, preferred_element_type=F32))

    p_scr[...] = jnp.zeros(p_scr.shape, BF16)
    alpha_scr[...] = jnp.ones(alpha_scr.shape, F32)
    m_scr[...] = jnp.full(m_scr.shape, _MASKED, F32)
    l_scr[...] = jnp.zeros(l_scr.shape, F32)
    acc_scr[...] = jnp.zeros(acc_scr.shape, F32)
    scores(0, causal)

    @pl.loop(1, qi + 1)
    def _(i):
        accumulate(i - 2)
        softmax_update()
        scores(i)

    accumulate(qi - 1)
    softmax_update()
    accumulate(qi)
    lam = (jnp.exp(jnp.sum(lq1_ref[...] * lk1_ref[...], axis=-1, keepdims=True))
           - jnp.exp(jnp.sum(lq2_ref[...] * lk2_ref[...], axis=-1, keepdims=True))
           + lam_init)
    o_ref[...] = acc_scr[0] / l_scr[0] - lam * (acc_scr[1] / l_scr[1])


def _diff_attn(lam_vecs, q, k, v, blk, lam_init):
    b, seq, _ = q.shape
    lam_spec = pl.BlockSpec((1, DIFF_HEAD_DIM), lambda bi, h, qi: (0, 0))
    kv_spec = pl.BlockSpec((None, seq, LANES), lambda bi, h, qi: (bi, 0, h))
    return pl.pallas_call(
        functools.partial(_diff_attn_kernel, blk=blk, lam_init=lam_init),
        grid=(b, DIFF_HEADS, seq // blk),
        in_specs=[lam_spec] * 4 + [
            pl.BlockSpec((None, blk, LANES), lambda bi, h, qi: (bi, qi, h)),
            kv_spec, kv_spec,
        ],
        out_specs=pl.BlockSpec((None, blk, LANES), lambda bi, h, qi: (bi, qi, h)),
        out_shape=jax.ShapeDtypeStruct((b, seq, BRANCH_W), F32),
        scratch_shapes=[pltpu.VMEM((2, blk, blk), F32), pltpu.VMEM((2, blk, blk), BF16)]
                       + [pltpu.VMEM((2, blk, LANES), F32)] * 4,
        compiler_params=pltpu.CompilerParams(
            dimension_semantics=("arbitrary", "arbitrary", "arbitrary"),
            vmem_limit_bytes=VMEM_LIMIT),
        name="diff_attn",
    )(*lam_vecs, q, k, v)


def _merge_out_kernel(x_ref, ob_ref, oc_ref, p_ref, nw_ref, wc_ref, wz_ref, wg_ref, bg_ref,
                      cw_ref, sub_ref, woc_ref, wos_ref, wod_ref, wo_ref, pn_ref, wpg_ref,
                      wp_ref, out_ref, tail_ref, *, tiles_per_seq, lam_init):
    tm = x_ref.shape[0]

    @pl.when(pl.program_id(0) % tiles_per_seq == 0)
    def _():
        tail_ref[...] = jnp.zeros_like(tail_ref)

    x = x_ref[...]
    xn = _rms_scale(x, nw_ref[...]).astype(BF16)

    def mm(a, w):
        return jnp.dot(a, w, preferred_element_type=F32)

    c_b = mm(xn, wc_ref[:, 0 * CONV_DIM:1 * CONV_DIM])
    u = mm(xn, wc_ref[:, 1 * CONV_DIM:2 * CONV_DIM]) * mm(xn, wc_ref[:, 2 * CONV_DIM:3 * CONV_DIM])
    ext = jnp.concatenate([tail_ref[...], u], axis=0)
    tail_ref[...] = u[tm - SUBLANES:, :]
    conv = cw_ref[CONV_WIDTH - 1:CONV_WIDTH, :] * u
    for j in range(1, CONV_WIDTH):
        conv = conv + (cw_ref[CONV_WIDTH - 1 - j:CONV_WIDTH - j, :]
                       * pltpu.roll(ext, j, 0)[SUBLANES:, :])
    c_z = mm(xn, wc_ref[:, 3 * CONV_DIM:4 * CONV_DIM])
    y_a = mm((c_b * conv * _silu(c_z)).astype(BF16), woc_ref[...])

    s_z = mm(xn, wz_ref[:, 0:BRANCH_W])
    y_b = mm((ob_ref[...] * _silu(s_z)).astype(BF16), wos_ref[...])

    oc = oc_ref[...]
    sub = sub_ref[...]
    heads = []
    for h in range(DIFF_HEADS):
        heads.append(_rms_scale(oc[:, h * DIFF_V_DIM:(h + 1) * DIFF_V_DIM], sub) * (1.0 - lam_init))
    oc_n = jnp.concatenate(heads, axis=1)
    d_z = mm(xn, wz_ref[:, BRANCH_W:2 * BRANCH_W])
    y_c = mm((oc_n * _silu(d_z)).astype(BF16), wod_ref[...])

    h_mix = None
    for n, y in enumerate((y_a, y_b, y_c)):
        g = _sigmoid(mm(xn, wg_ref[:, n * D_MODEL:(n + 1) * D_MODEL])
                     + bg_ref[:, n * D_MODEL:(n + 1) * D_MODEL])
        h_mix = g * y if h_mix is None else h_mix + g * y
    x1 = x + mm(h_mix.astype(BF16), wo_ref[...])

    pg = _sigmoid(mm(_rms_scale(x1, pn_ref[...]).astype(BF16), wpg_ref[...]))
    out_ref[...] = x1 + pg * mm(p_ref[...].astype(BF16), wp_ref[...])


def _merge_out(x2d, ob, oc, p2d, norm_w, w_conv, w_z, w_g, b_gate, conv_w, subln, w_out_conv,
               w_out_sb, w_out_diff, w_out, ple_norm, w_ple_gate, w_ple, seq, tm, lam_init):
    m = x2d.shape[0]
    const = lambda i: (0, 0)

    def resident(a):
        return pl.BlockSpec(a.shape, const, pipeline_mode=pl.Buffered(1))

    def rows(width):
        return pl.BlockSpec((tm, width), lambda i: (i, 0))

    weights = (norm_w, w_conv, w_z, w_g, b_gate, conv_w, subln, w_out_conv, w_out_sb,
               w_out_diff, w_out, ple_norm, w_ple_gate, w_ple)
    return pl.pallas_call(
        functools.partial(_merge_out_kernel, tiles_per_seq=seq // tm, lam_init=lam_init),
        grid=(m // tm,),
        in_specs=[rows(D_MODEL), rows(BRANCH_W), rows(BRANCH_W), rows(PLE_DIM)]
                 + [resident(a) for a in weights],
        out_specs=rows(D_MODEL),
        out_shape=jax.ShapeDtypeStruct((m, D_MODEL), F32),
        scratch_shapes=[pltpu.VMEM((SUBLANES, CONV_DIM), F32)],
        compiler_params=pltpu.CompilerParams(
            dimension_semantics=("arbitrary",), vmem_limit_bytes=VMEM_LIMIT),
        name="merge_out",
    )(x2d, ob, oc, p2d, *weights)


def _rope_tables(seq):
    pos = jnp.arange(seq, dtype=F32)
    inv = 1.0 / (ROPE_THETA ** (jnp.arange(0, DIFF_HEAD_DIM, 2, dtype=F32) / DIFF_HEAD_DIM))
    ang = pos[:, None] * inv[None, :]
    cos, sin = jnp.cos(ang), jnp.sin(ang)
    reps = LANES // DIFF_HEAD_DIM
    cos_t = jnp.concatenate([cos, cos] * reps, axis=1)
    sin_t = jnp.concatenate([-sin, sin] * reps, axis=1)
    return cos_t, sin_t


def kernel(x, p, norm_w, w_in, b_gate, conv_w, diff_q_norm, diff_k_norm, lam_q1, lam_k1, lam_q2,
           lam_k2, diff_subln, w_out_conv, w_out_sb, w_out_diff, w_out, ple_norm, w_ple_gate,
           w_ple):
    b, seq, _ = x.shape
    depth = w_in.shape[0]
    tm_pre, tm_post, q_blk = _tiles(seq)
    cos_t, sin_t = _rope_tables(seq)
    idx = jnp.arange(MXU_DIM)
    seg = (idx[:, None] // DIFF_HEAD_DIM == idx[None, :] // DIFF_HEAD_DIM).astype(BF16)
    kidx = jnp.arange(q_blk)
    tri = (kidx[:, None] >= kidx[None, :]).astype(BF16)
    row = lambda a: a.reshape(1, -1)

    x2d = x.reshape(b * seq, D_MODEL)
    for i in range(depth):
        lam_init = 0.8 - 0.6 * math.exp(-0.3 * i)
        w = w_in[i].astype(BF16)
        w_attn = jnp.concatenate([w[:, _OFF_SQ:_OFF_SZ], w[:, _OFF_DQ:_OFF_DZ]], axis=1)
        w_z = jnp.concatenate([w[:, _OFF_SZ:_OFF_DQ], w[:, _OFF_DZ:_OFF_G]], axis=1)
        n_maps = BRANCH_W // DIFF_HEAD_DIM

        sq, sk, sv, dq, dk, dv = _attn_in_proj(
            x2d, row(norm_w[i]), w_attn, cos_t, sin_t,
            row(jnp.tile(diff_q_norm[i], n_maps)), row(jnp.tile(diff_k_norm[i], n_maps)),
            seg, seq, tm_pre)
        shp = (b, seq, BRANCH_W)
        ob = _sb_attn(sq.reshape(shp), sk.reshape(shp), sv.reshape(shp), tri, q_blk)
        oc = _diff_attn((row(lam_q1[i]), row(lam_k1[i]), row(lam_q2[i]), row(lam_k2[i])),
                        dq.reshape(shp), dk.reshape(shp), dv.reshape(shp), q_blk, lam_init)
        x2d = _merge_out(
            x2d, ob.reshape(b * seq, BRANCH_W), oc.reshape(b * seq, BRANCH_W),
            p[i].reshape(b * seq, PLE_DIM), row(norm_w[i]), w[:, _OFF_CONV:_OFF_SQ], w_z,
            w[:, _OFF_G:], row(b_gate[i]), conv_w[i], row(diff_subln[i]),
            w_out_conv[i].astype(BF16), w_out_sb[i].astype(BF16), w_out_diff[i].astype(BF16),
            w_out[i].astype(BF16), row(ple_norm[i]), w_ple_gate[i].astype(BF16),
            w_ple[i].astype(BF16), seq, tm_post, lam_init)
    return x2d.reshape(b, seq, D_MODEL)
```

```python
import functools
import math

import jax
import jax.numpy as jnp
from jax import lax
from jax.experimental import pallas as pl
from jax.experimental.pallas import tpu as pltpu

F32 = jnp.float32
BF16 = jnp.bfloat16

D_MODEL = 1024
PLE_DIM = 256
N_BRANCH = 3
CONV_DIM = 512
CONV_WIDTH = 3
SB_HEADS = 8
SB_HEAD_DIM = 64
DIFF_HEADS = 4
DIFF_HEAD_DIM = 64
DIFF_V_DIM = 2 * DIFF_HEAD_DIM
ROPE_THETA = 10000.0
EPS = 1e-6

BRANCH_W = 512
LANES = 128
SUBLANES = 8
MXU_DIM = 256
VMEM_LIMIT = 56 * 1024 * 1024
_MASKED = -1e30
_LOG2E = math.log2(math.e)
SB_HEADS_PER_STEP = 4
DIFF_HEADS_PER_STEP = 2

_OFF_CONV = 0
_OFF_SQ = 4 * CONV_DIM
_OFF_SZ = _OFF_SQ + 3 * BRANCH_W
_OFF_DQ = _OFF_SZ + BRANCH_W
_OFF_DZ = _OFF_DQ + 3 * BRANCH_W
_OFF_G = _OFF_DZ + BRANCH_W


def _tiles(seq):
    tm_pre = min(512, seq)
    tm_post = min(256, seq)
    q_blk = min(MXU_DIM, seq)
    assert seq % tm_pre == 0 and seq % tm_post == 0 and seq % q_blk == 0
    return tm_pre, tm_post, q_blk


def _split_bf16(a):
    hi = a.astype(BF16)
    lo = (a - hi.astype(F32)).astype(BF16)
    return hi, lo


def _rms_scale(x, w):
    ms = jnp.mean(x * x, axis=-1, keepdims=True)
    return x * lax.rsqrt(ms + EPS) * w


def _sigmoid(a):
    return 1.0 / (1.0 + jnp.exp(-a))


def _silu(a):
    return a * _sigmoid(a)


def _attn_in_proj_kernel(x_ref, nw_ref, w_ref, cos_ref, sin_ref, qn_ref, kn_ref, seg_ref,
                         sq_ref, sk_ref, sv_ref, dq_ref, dk_ref, dv_ref):
    xn = _rms_scale(x_ref[...], nw_ref[...]).astype(BF16)

    def proj(c):
        return jnp.dot(xn, w_ref[:, c * BRANCH_W:(c + 1) * BRANCH_W],
                       preferred_element_type=F32)

    sq_ref[...] = (proj(0) * (SB_HEAD_DIM ** -0.5)).astype(BF16)
    sk_ref[...] = proj(1).astype(BF16)
    sv_ref[...] = proj(2).astype(BF16)

    reps = BRANCH_W // LANES
    cos = jnp.concatenate([cos_ref[...]] * reps, axis=1)
    sin = jnp.concatenate([sin_ref[...]] * reps, axis=1)
    lane = lax.broadcasted_iota(jnp.int32, cos.shape, 1)
    first_half = (lane & (DIFF_HEAD_DIM - 1)) < (DIFF_HEAD_DIM // 2)
    seg = seg_ref[...]

    def norm_rope(y, wn):
        hi, lo = _split_bf16(y * y)
        parts = []
        for c in range(BRANCH_W // MXU_DIM):
            sl = slice(c * MXU_DIM, (c + 1) * MXU_DIM)
            parts.append(jnp.dot(hi[:, sl], seg, preferred_element_type=F32)
                         + jnp.dot(lo[:, sl], seg, preferred_element_type=F32))
        ms = jnp.concatenate(parts, axis=1) * (1.0 / DIFF_HEAD_DIM)
        yn = y * lax.rsqrt(ms + EPS) * wn
        half = DIFF_HEAD_DIM // 2
        partner = jnp.where(first_half,
                            pltpu.roll(yn, BRANCH_W - half, 1),
                            pltpu.roll(yn, half, 1))
        return yn * cos + partner * sin

    dq_ref[...] = (norm_rope(proj(3), qn_ref[...]) * (DIFF_HEAD_DIM ** -0.5)).astype(BF16)
    dk_ref[...] = norm_rope(proj(4), kn_ref[...]).astype(BF16)
    dv_ref[...] = proj(5).astype(BF16)


def _attn_in_proj(x2d, norm_w, w_attn, cos_t, sin_t, qn, kn, seg, seq, tm):
    m = x2d.shape[0]
    n_seq_tiles = seq // tm
    const = lambda i: (0, 0)
    out = jax.ShapeDtypeStruct((m, BRANCH_W), BF16)
    out_spec = pl.BlockSpec((tm, BRANCH_W), lambda i: (i, 0))
    return pl.pallas_call(
        _attn_in_proj_kernel,
        grid=(m // tm,),
        in_specs=[
            pl.BlockSpec((tm, D_MODEL), lambda i: (i, 0)),
            pl.BlockSpec((1, D_MODEL), const),
            pl.BlockSpec((D_MODEL, 6 * BRANCH_W), const),
            pl.BlockSpec((tm, LANES), lambda i: (i % n_seq_tiles, 0)),
            pl.BlockSpec((tm, LANES), lambda i: (i % n_seq_tiles, 0)),
            pl.BlockSpec((1, BRANCH_W), const),
            pl.BlockSpec((1, BRANCH_W), const),
            pl.BlockSpec((MXU_DIM, MXU_DIM), const),
        ],
        out_specs=[out_spec] * 6,
        out_shape=[out] * 6,
        compiler_params=pltpu.CompilerParams(
            dimension_semantics=("arbitrary",), vmem_limit_bytes=VMEM_LIMIT),
        name="attn_in_proj",
    )(x2d, norm_w, w_attn, cos_t, sin_t, qn, kn, seg)


def _sb_attn_kernel(q_ref, k_ref, v_ref, tri_ref, o_ref,
                    z_scr, d_scr, tot_scr, run_scr, acc_scr, *, blk):
    qi = pl.program_id(2)
    n_heads = z_scr.shape[0]
    lane = lax.broadcasted_iota(jnp.int32, (blk, LANES), 1)
    first = lane < SB_HEAD_DIM
    q_heads = []
    for g in range(n_heads // 2):
        qf = q_ref[:, g * LANES:(g + 1) * LANES].astype(F32)
        q_heads += [jnp.where(first, qf, 0.0).astype(BF16), jnp.where(first, 0.0, qf).astype(BF16)]
    tri2 = tri_ref[...]
    row = lax.broadcasted_iota(jnp.int32, (blk, blk), 0)
    col = lax.broadcasted_iota(jnp.int32, (blk, blk), 1)
    strictly_past = col < row


    def key_start(t):
        return pl.multiple_of((qi - jnp.maximum(t, 0)) * blk, blk)

    def slab(ref, t, h):
        g = h // 2
        return ref[pl.ds(key_start(t), blk), g * LANES:(g + 1) * LANES]

    def logits(t, mask=None):
        for h in range(n_heads):
            z = lax.dot_general(q_heads[h], slab(k_ref, t, h), (((1,), (1,)), ((), ())),
                                preferred_element_type=F32)
            z_scr[h] = z if mask is None else jnp.where(mask, z, _MASKED)

    def minus_local_cumsum():
        for h in range(n_heads):
            z = z_scr[h]
            sp = jnp.maximum(z, 0.0) + jnp.log(1.0 + jnp.exp2(jnp.abs(z) * (-_LOG2E)))
            hi, lo = _split_bf16(sp)
            c = jnp.dot(jnp.concatenate([hi, lo], axis=1), tri2, preferred_element_type=F32)
            d_scr[h] = z - c
            tot_scr[h] = jnp.broadcast_to(c[:, 0:1], (blk, LANES))

    def accumulate(t):
        for h in range(n_heads):
            run = run_scr[h]
            w = jnp.exp(d_scr[h] - jnp.concatenate([run] * (blk // LANES), axis=1))
            acc_scr[h] += jnp.dot(w.astype(BF16), slab(v_ref, t, h), preferred_element_type=F32)
            run_scr[h] = run + tot_scr[h]

    d_scr[...] = jnp.full(d_scr.shape, _MASKED, F32)
    tot_scr[...] = jnp.zeros(tot_scr.shape, F32)
    run_scr[...] = jnp.zeros(run_scr.shape, F32)
    acc_scr[...] = jnp.zeros(acc_scr.shape, F32)
    logits(0, strictly_past)

    @pl.loop(1, qi + 1)
    def _(i):
        accumulate(i - 2)
        minus_local_cumsum()
        logits(i)

    accumulate(qi - 1)
    minus_local_cumsum()
    accumulate(qi)
    for g in range(n_heads // 2):
        o_ref[:, g * LANES:(g + 1) * LANES] = jnp.where(first, acc_scr[2 * g], acc_scr[2 * g + 1])


def _sb_attn(q, k, v, tri2, blk, n_heads):
    b, seq, _ = q.shape
    width = n_heads * SB_HEAD_DIM
    kv_spec = pl.BlockSpec((None, seq, width), lambda bi, hg, qi: (bi, 0, hg))
    tile_scr = pltpu.VMEM((n_heads, blk, blk), F32)
    row_scr = pltpu.VMEM((n_heads, blk, LANES), F32)
    return pl.pallas_call(
        functools.partial(_sb_attn_kernel, blk=blk),
        grid=(b, BRANCH_W // width, seq // blk),
        in_specs=[
            pl.BlockSpec((None, blk, width), lambda bi, hg, qi: (bi, qi, hg)),
            kv_spec, kv_spec,
            pl.BlockSpec((2 * blk, blk), lambda bi, hg, qi: (0, 0)),
        ],
        out_specs=pl.BlockSpec((None, blk, width), lambda bi, hg, qi: (bi, qi, hg)),
        out_shape=jax.ShapeDtypeStruct((b, seq, BRANCH_W), F32),
        scratch_shapes=[tile_scr, tile_scr, row_scr, row_scr, row_scr],
        compiler_params=pltpu.CompilerParams(
            dimension_semantics=("arbitrary", "arbitrary", "arbitrary"),
            vmem_limit_bytes=VMEM_LIMIT),
        name="sb_attn",
    )(q, k, v, tri2)


def _diff_attn_kernel(lq1_ref, lk1_ref, lq2_ref, lk2_ref, q_ref, k_ref, v_ref, o_ref,
                      s_scr, p_scr, alpha_scr, m_scr, l_scr, acc_scr, *, blk, lam_init):
    qi = pl.program_id(2)
    n_maps = s_scr.shape[0]
    lane = lax.broadcasted_iota(jnp.int32, (blk, LANES), 1)
    first = lane < DIFF_HEAD_DIM
    q_maps = []
    for g in range(n_maps // 2):
        qf = q_ref[:, g * LANES:(g + 1) * LANES].astype(F32)
        q_maps += [jnp.where(first, qf, 0.0).astype(BF16), jnp.where(first, 0.0, qf).astype(BF16)]
    row = lax.broadcasted_iota(jnp.int32, (blk, blk), 0)
    col = lax.broadcasted_iota(jnp.int32, (blk, blk), 1)
    causal = col <= row


    def key_start(t):
        return pl.multiple_of((qi - jnp.maximum(t, 0)) * blk, blk)

    def slab(ref, t, c):
        g = c // 2
        return ref[pl.ds(key_start(t), blk), g * LANES:(g + 1) * LANES]

    def scores(t, mask=None):
        for c in range(n_maps):
            s = lax.dot_general(q_maps[c], slab(k_ref, t, c), (((1,), (1,)), ((), ())),
                                preferred_element_type=F32)
            s_scr[c] = s if mask is None else jnp.where(mask, s, -jnp.inf)

    def row_stat(a):
        return jnp.broadcast_to(a, (blk, LANES))

    def softmax_update():
        for c in range(n_maps):
            s = s_scr[c]
            m = m_scr[c]
            m_new = jnp.maximum(m, row_stat(jnp.max(s, axis=-1, keepdims=True)))
            alpha = jnp.exp(m - m_new)
            p = jnp.exp(s - jnp.concatenate([m_new] * (blk // LANES), axis=1))
            p_scr[c] = p.astype(BF16)
            alpha_scr[c] = alpha
            m_scr[c] = m_new
            l_scr[c] = alpha * l_scr[c] + sum(
                p[:, j * LANES:(j + 1) * LANES] for j in range(blk // LANES))

    def accumulate(t):
        for c in range(n_maps):
            acc_scr[c] = (alpha_scr[c] * acc_scr[c]
                          + jnp.dot(p_scr[c], slab(v_ref, t, c), preferred_element_type=F32))

    p_scr[...] = jnp.zeros(p_scr.shape, BF16)
    alpha_scr[...] = jnp.ones(alpha_scr.shape, F32)
    m_scr[...] = jnp.full(m_scr.shape, _MASKED, F32)
    l_scr[...] = jnp.zeros(l_scr.shape, F32)
    acc_scr[...] = jnp.zeros(acc_scr.shape, F32)
    scores(0, causal)

    @pl.loop(1, qi + 1)
    def _(i):
        accumulate(i - 2)
        softmax_update()
        scores(i)

    accumulate(qi - 1)
    softmax_update()
    accumulate(qi)
    lam = (jnp.exp(jnp.sum(lq1_ref[...] * lk1_ref[...], axis=-1, keepdims=True))
           - jnp.exp(jnp.sum(lq2_ref[...] * lk2_ref[...], axis=-1, keepdims=True))
           + lam_init)
    def normalised(c):
        return acc_scr[c] / row_stat(jnp.sum(l_scr[c], axis=-1, keepdims=True))

    for g in range(n_maps // 2):
        o_ref[:, g * LANES:(g + 1) * LANES] = normalised(2 * g) - lam * normalised(2 * g + 1)


def _diff_attn(lam_vecs, q, k, v, blk, lam_init, n_heads):
    b, seq, _ = q.shape
    width = n_heads * DIFF_V_DIM
    n_maps = 2 * n_heads
    lam_spec = pl.BlockSpec((1, DIFF_HEAD_DIM), lambda bi, hg, qi: (0, 0))
    kv_spec = pl.BlockSpec((None, seq, width), lambda bi, hg, qi: (bi, 0, hg))
    return pl.pallas_call(
        functools.partial(_diff_attn_kernel, blk=blk, lam_init=lam_init),
        grid=(b, BRANCH_W // width, seq // blk),
        in_specs=[lam_spec] * 4 + [
            pl.BlockSpec((None, blk, width), lambda bi, hg, qi: (bi, qi, hg)),
            kv_spec, kv_spec,
        ],
        out_specs=pl.BlockSpec((None, blk, width), lambda bi, hg, qi: (bi, qi, hg)),
        out_shape=jax.ShapeDtypeStruct((b, seq, BRANCH_W), F32),
        scratch_shapes=[pltpu.VMEM((n_maps, blk, blk), F32), pltpu.VMEM((n_maps, blk, blk), BF16)]
                       + [pltpu.VMEM((n_maps, blk, LANES), F32)] * 4,
        compiler_params=pltpu.CompilerParams(
            dimension_semantics=("arbitrary", "arbitrary", "arbitrary"),
            vmem_limit_bytes=VMEM_LIMIT),
        name="diff_attn",
    )(*lam_vecs, q, k, v)


def _merge_out_kernel(x_ref, ob_ref, oc_ref, p_ref, nw_ref, wc_ref, wz_ref, wg_ref, bg_ref,
                      cw_ref, sub_ref, woc_ref, wos_ref, wod_ref, wo_ref, pn_ref, wpg_ref,
                      wp_ref, out_ref, tail_ref, *, tiles_per_seq, lam_init):
    tm = x_ref.shape[0]

    @pl.when(pl.program_id(0) % tiles_per_seq == 0)
    def _():
        tail_ref[...] = jnp.zeros_like(tail_ref)

    x = x_ref[...]
    xn = _rms_scale(x, nw_ref[...]).astype(BF16)

    def mm(a, w):
        return jnp.dot(a, w, preferred_element_type=F32)

    c_b = mm(xn, wc_ref[:, 0 * CONV_DIM:1 * CONV_DIM])
    u = mm(xn, wc_ref[:, 1 * CONV_DIM:2 * CONV_DIM]) * mm(xn, wc_ref[:, 2 * CONV_DIM:3 * CONV_DIM])
    ext = jnp.concatenate([tail_ref[...], u], axis=0)
    tail_ref[...] = u[tm - SUBLANES:, :]
    conv = cw_ref[CONV_WIDTH - 1:CONV_WIDTH, :] * u
    for j in range(1, CONV_WIDTH):
        conv = conv + (cw_ref[CONV_WIDTH - 1 - j:CONV_WIDTH - j, :]
                       * pltpu.roll(ext, j, 0)[SUBLANES:, :])
    c_z = mm(xn, wc_ref[:, 3 * CONV_DIM:4 * CONV_DIM])
    y_a = mm((c_b * conv * _silu(c_z)).astype(BF16), woc_ref[...])

    s_z = mm(xn, wz_ref[:, 0:BRANCH_W])
    y_b = mm((ob_ref[...] * _silu(s_z)).astype(BF16), wos_ref[...])

    oc = oc_ref[...]
    sub = sub_ref[...]
    heads = []
    for h in range(DIFF_HEADS):
        heads.append(_rms_scale(oc[:, h * DIFF_V_DIM:(h + 1) * DIFF_V_DIM], sub) * (1.0 - lam_init))
    oc_n = jnp.concatenate(heads, axis=1)
    d_z = mm(xn, wz_ref[:, BRANCH_W:2 * BRANCH_W])
    y_c = mm((oc_n * _silu(d_z)).astype(BF16), wod_ref[...])

    h_mix = None
    for n, y in enumerate((y_a, y_b, y_c)):
        g = _sigmoid(mm(xn, wg_ref[:, n * D_MODEL:(n + 1) * D_MODEL])
                     + bg_ref[:, n * D_MODEL:(n + 1) * D_MODEL])
        h_mix = g * y if h_mix is None else h_mix + g * y
    x1 = x + mm(h_mix.astype(BF16), wo_ref[...])

    pg = _sigmoid(mm(_rms_scale(x1, pn_ref[...]).astype(BF16), wpg_ref[...]))
    out_ref[...] = x1 + pg * mm(p_ref[...].astype(BF16), wp_ref[...])


def _merge_out(x2d, ob, oc, p2d, norm_w, w_conv, w_z, w_g, b_gate, conv_w, subln, w_out_conv,
               w_out_sb, w_out_diff, w_out, ple_norm, w_ple_gate, w_ple, seq, tm, lam_init):
    m = x2d.shape[0]
    const = lambda i: (0, 0)

    def resident(a):
        return pl.BlockSpec(a.shape, const, pipeline_mode=pl.Buffered(1))

    def rows(width):
        return pl.BlockSpec((tm, width), lambda i: (i, 0))

    weights = (norm_w, w_conv, w_z, w_g, b_gate, conv_w, subln, w_out_conv, w_out_sb,
               w_out_diff, w_out, ple_norm, w_ple_gate, w_ple)
    return pl.pallas_call(
        functools.partial(_merge_out_kernel, tiles_per_seq=seq // tm, lam_init=lam_init),
        grid=(m // tm,),
        in_specs=[rows(D_MODEL), rows(BRANCH_W), rows(BRANCH_W), rows(PLE_DIM)]
                 + [resident(a) for a in weights],
        out_specs=rows(D_MODEL),
        out_shape=jax.ShapeDtypeStruct((m, D_MODEL), F32),
        scratch_shapes=[pltpu.VMEM((SUBLANES, CONV_DIM), F32)],
        compiler_params=pltpu.CompilerParams(
            dimension_semantics=("arbitrary",), vmem_limit_bytes=VMEM_LIMIT),
        name="merge_out",
    )(x2d, ob, oc, p2d, *weights)


def _rope_tables(seq):
    pos = jnp.arange(seq, dtype=F32)
    inv = 1.0 / (ROPE_THETA ** (jnp.arange(0, DIFF_HEAD_DIM, 2, dtype=F32) / DIFF_HEAD_DIM))
    ang = pos[:, None] * inv[None, :]
    cos, sin = jnp.cos(ang), jnp.sin(ang)
    reps = LANES // DIFF_HEAD_DIM
    cos_t = jnp.concatenate([cos, cos] * reps, axis=1)
    sin_t = jnp.concatenate([-sin, sin] * reps, axis=1)
    return cos_t, sin_t


def kernel(x, p, norm_w, w_in, b_gate, conv_w, diff_q_norm, diff_k_norm, lam_q1, lam_k1, lam_q2,
           lam_k2, diff_subln, w_out_conv, w_out_sb, w_out_diff, w_out, ple_norm, w_ple_gate,
           w_ple):
    b, seq, _ = x.shape
    depth = w_in.shape[0]
    tm_pre, tm_post, q_blk = _tiles(seq)
    cos_t, sin_t = _rope_tables(seq)
    idx = jnp.arange(MXU_DIM)
    seg = (idx[:, None] // DIFF_HEAD_DIM == idx[None, :] // DIFF_HEAD_DIM).astype(BF16)
    kidx = jnp.arange(q_blk)
    tri = (kidx[:, None] >= kidx[None, :]).astype(BF16)
    tri2 = jnp.concatenate([tri, tri], axis=0)
    row = lambda a: a.reshape(1, -1)

    x2d = x.reshape(b * seq, D_MODEL)
    for i in range(depth):
        lam_init = 0.8 - 0.6 * math.exp(-0.3 * i)
        w = w_in[i].astype(BF16)
        w_attn = jnp.concatenate([w[:, _OFF_SQ:_OFF_SZ], w[:, _OFF_DQ:_OFF_DZ]], axis=1)
        w_z = jnp.concatenate([w[:, _OFF_SZ:_OFF_DQ], w[:, _OFF_DZ:_OFF_G]], axis=1)
        n_maps = BRANCH_W // DIFF_HEAD_DIM

        sq, sk, sv, dq, dk, dv = _attn_in_proj(
            x2d, row(norm_w[i]), w_attn, cos_t, sin_t,
            row(jnp.tile(diff_q_norm[i], n_maps)), row(jnp.tile(diff_k_norm[i], n_maps)),
            seg, seq, tm_pre)
        shp = (b, seq, BRANCH_W)
        ob = _sb_attn(sq.reshape(shp), sk.reshape(shp), sv.reshape(shp), tri2, q_blk,
                      SB_HEADS_PER_STEP)
        oc = _diff_attn((row(lam_q1[i]), row(lam_k1[i]), row(lam_q2[i]), row(lam_k2[i])),
                        dq.reshape(shp), dk.reshape(shp), dv.reshape(shp), q_blk, lam_init,
                        DIFF_HEADS_PER_STEP)
        x2d = _merge_out(
            x2d, ob.reshape(b * seq, BRANCH_W), oc.reshape(b * seq, BRANCH_W),
            p[i].reshape(b * seq, PLE_DIM), row(norm_w[i]), w[:, _OFF_CONV:_OFF_SQ], w_z,
            w[:, _OFF_G:], row(b_gate[i]), conv_w[i], row(diff_subln[i]),
            w_out_conv[i].astype(BF16), w_out_sb[i].astype(BF16), w_out_diff[i].astype(BF16),
            w_out[i].astype(BF16), row(ple_norm[i]), w_ple_gate[i].astype(BF16),
            w_ple[i].astype(BF16), seq, tm_post, lam_init)
    return x2d.reshape(b, seq, D_MODEL)
```
